```python
import jax, jax.numpy as jnp
from jax import lax
import numpy as np

D_MODEL = 1024
BATCH = 8
SEQ = 2048
DEPTH = 1
DEC_BATCH = 128
DEC_SEQ = 1
PAST_LEN = 16384
PAGE_SIZE = 128

N_META = 16
D_CONV = D_MODEL
CONV_A_W = 3
D_RNN = D_MODEL
N_RNN_HEADS = 16
RNN_HEAD_DIM = D_RNN // N_RNN_HEADS
CONV_B_W = 4
RG_C = 8.0
D_FF = 2816
D_IN = 3 * D_CONV + 2 * D_RNN + 2 * D_MODEL
EPS = 1e-6

kernel_name = "hybrid_shortconv_rglru_macaron_step"


def _rmsnorm(x, g):
    xf = x.astype(jnp.float32)
    y = xf * lax.rsqrt(jnp.mean(xf * xf, axis=-1, keepdims=True) + EPS) * g.astype(jnp.float32)
    return y.astype(x.dtype)


def _swiglu(x, w_gate, w_up, w_down):
    return (jax.nn.silu(x @ w_gate) * (x @ w_up)) @ w_down


def _causal_dwconv(x, buf, w):
    T = x.shape[1]
    K = w.shape[0]
    xp = jnp.concatenate([buf.astype(x.dtype), x], axis=1)
    y = xp[:, 0:T] * w[0]
    for k in range(1, K):
        y = y + xp[:, k:k + T] * w[k]
    return y, xp[:, xp.shape[1] - (K - 1):]


def _rglru(x, h0, w_r, b_r, w_i, b_i, lam):
    Bsz, T, _ = x.shape
    xh = x.reshape(Bsz, T, N_RNN_HEADS, RNN_HEAD_DIM)
    r = jax.nn.sigmoid((jnp.einsum('bthi,hij->bthj', xh, w_r).reshape(Bsz, T, D_RNN) + b_r).astype(jnp.float32))
    i = jax.nn.sigmoid((jnp.einsum('bthi,hij->bthj', xh, w_i).reshape(Bsz, T, D_RNN) + b_i).astype(jnp.float32))
    log_a = -RG_C * r * jax.nn.softplus(-lam.astype(jnp.float32))
    a = jnp.exp(log_a)
    u = jnp.sqrt(-jnp.expm1(2.0 * log_a)) * (i * x.astype(jnp.float32))

    def step(h, au):
        a_t, u_t = au
        h = a_t * h + u_t
        return h, h

    h_last, hs = lax.scan(step, h0, (jnp.swapaxes(a, 0, 1), jnp.swapaxes(u, 0, 1)))
    return jnp.swapaxes(hs, 0, 1).astype(x.dtype), h_last


def _mixer(h, buf_a, buf_b, h0, p):
    u = h @ p['w_in']
    idx = [D_CONV, 2 * D_CONV, 3 * D_CONV, 3 * D_CONV + D_RNN,
           3 * D_CONV + 2 * D_RNN, 3 * D_CONV + 2 * D_RNN + D_MODEL]
    a_b, a_c, a_x, b_x, b_gate, g_a, g_b = jnp.split(u, idx, axis=-1)
    conv_a, new_buf_a = _causal_dwconv(a_c * a_x, buf_a, p['conv_a_w'])
    y_a = (a_b * conv_a) @ p['w_out_a']
    conv_b, new_buf_b = _causal_dwconv(b_x, buf_b, p['conv_b_w'])
    conv_b = conv_b + p['conv_b_b']
    rg, h_last = _rglru(conv_b, h0, p['w_rg_r'], p['b_rg_r'], p['w_rg_i'], p['b_rg_i'], p['rg_lambda'])
    y_b = (jax.nn.gelu(b_gate, approximate=True) * rg) @ p['w_out_b']
    merged = jax.nn.sigmoid(g_a) * y_a + jax.nn.sigmoid(g_b) * y_b
    return merged @ p['w_o'], new_buf_a, new_buf_b, h_last


def _layer(x, buf_a, buf_b, h0, p):
    x = x + 0.5 * _rmsnorm(_swiglu(_rmsnorm(x, p['g_ffn1_pre']), p['w_ffn1_gate'], p['w_ffn1_up'], p['w_ffn1_down']), p['g_ffn1_post'])
    m, nba, nbb, hl = _mixer(_rmsnorm(x, p['g_mix_pre']), buf_a, buf_b, h0, p)
    x = x + _rmsnorm(m, p['g_mix_post'])
    x = x + 0.5 * _rmsnorm(_swiglu(_rmsnorm(x, p['g_ffn2_pre']), p['w_ffn2_gate'], p['w_ffn2_up'], p['w_ffn2_down']), p['g_ffn2_post'])
    return x, nba, nbb, hl


def setup_inputs(seed: int = 0) -> dict:
    key = jax.random.key(seed)
    ks = iter(jax.random.split(key, 40))
    f32 = jnp.float32

    def nrm(shape, scale):
        return jax.random.normal(next(ks), shape, f32) * scale

    def gain(shape):
        return 1.0 + 0.05 * jax.random.normal(next(ks), shape, f32)

    L = DEPTH
    ac = jax.random.uniform(next(ks), (L, D_RNN), f32, 0.9, 0.999)
    a0 = ac ** (1.0 / RG_C)
    lam = jnp.log(a0) - jnp.log1p(-a0)
    return {
        'x_prompt': nrm((BATCH, SEQ, D_MODEL), 1.0),
        'x_sample': nrm((DEC_BATCH, DEC_SEQ, D_MODEL), 1.0),
        'state_conv_a': nrm((L, DEC_BATCH, CONV_A_W - 1, D_CONV), 1.0),
        'state_conv_b': nrm((L, DEC_BATCH, CONV_B_W - 1, D_RNN), 1.0),
        'state_rglru': nrm((L, DEC_BATCH, D_RNN), 0.5),
        'meta_tokens': nrm((N_META, D_MODEL), 1.0),
        'g_ffn1_pre': gain((L, D_MODEL)),
        'g_ffn1_post': gain((L, D_MODEL)),
        'w_ffn1_gate': nrm((L, D_MODEL, D_FF), D_MODEL ** -0.5),
        'w_ffn1_up': nrm((L, D_MODEL, D_FF), D_MODEL ** -0.5),
        'w_ffn1_down': nrm((L, D_FF, D_MODEL), D_FF ** -0.5),
        'g_mix_pre': gain((L, D_MODEL)),
        'g_mix_post': gain((L, D_MODEL)),
        'w_in': nrm((L, D_MODEL, D_IN), D_MODEL ** -0.5),
        'conv_a_w': nrm((L, CONV_A_W, D_CONV), CONV_A_W ** -0.5),
        'w_out_a': nrm((L, D_CONV, D_MODEL), D_CONV ** -0.5),
        'conv_b_w': nrm((L, CONV_B_W, D_RNN), CONV_B_W ** -0.5),
        'conv_b_b': nrm((L, D_RNN), 0.02),
        'w_rg_r': nrm((L, N_RNN_HEADS, RNN_HEAD_DIM, RNN_HEAD_DIM), RNN_HEAD_DIM ** -0.5),
        'b_rg_r': nrm((L, D_RNN), 0.02),
        'w_rg_i': nrm((L, N_RNN_HEADS, RNN_HEAD_DIM, RNN_HEAD_DIM), RNN_HEAD_DIM ** -0.5),
        'b_rg_i': nrm((L, D_RNN), 0.02),
        'rg_lambda': lam,
        'w_out_b': nrm((L, D_RNN, D_MODEL), D_RNN ** -0.5),
        'w_o': nrm((L, D_MODEL, D_MODEL), D_MODEL ** -0.5),
        'g_ffn2_pre': gain((L, D_MODEL)),
        'g_ffn2_post': gain((L, D_MODEL)),
        'w_ffn2_gate': nrm((L, D_MODEL, D_FF), D_MODEL ** -0.5),
        'w_ffn2_up': nrm((L, D_MODEL, D_FF), D_MODEL ** -0.5),
        'w_ffn2_down': nrm((L, D_FF, D_MODEL), D_FF ** -0.5),
    }


def reference(x_prompt, x_sample, state_conv_a, state_conv_b, state_rglru, meta_tokens,
              g_ffn1_pre, g_ffn1_post, w_ffn1_gate, w_ffn1_up, w_ffn1_down,
              g_mix_pre, g_mix_post, w_in, conv_a_w, w_out_a, conv_b_w, conv_b_b,
              w_rg_r, b_rg_r, w_rg_i, b_rg_i, rg_lambda, w_out_b, w_o,
              g_ffn2_pre, g_ffn2_post, w_ffn2_gate, w_ffn2_up, w_ffn2_down):
    Bp = x_prompt.shape[0]
    meta = jnp.broadcast_to(meta_tokens.astype(x_prompt.dtype)[None], (Bp, N_META, D_MODEL))
    xp = jnp.concatenate([meta, x_prompt], axis=1)
    xs = x_sample
    pa, pb, ph, sa, sb, sh = [], [], [], [], [], []
    for l in range(DEPTH):
        p = {
            'g_ffn1_pre': g_ffn1_pre[l], 'g_ffn1_post': g_ffn1_post[l],
            'w_ffn1_gate': w_ffn1_gate[l], 'w_ffn1_up': w_ffn1_up[l], 'w_ffn1_down': w_ffn1_down[l],
            'g_mix_pre': g_mix_pre[l], 'g_mix_post': g_mix_post[l], 'w_in': w_in[l],
            'conv_a_w': conv_a_w[l], 'w_out_a': w_out_a[l], 'conv_b_w': conv_b_w[l], 'conv_b_b': conv_b_b[l],
            'w_rg_r': w_rg_r[l], 'b_rg_r': b_rg_r[l], 'w_rg_i': w_rg_i[l], 'b_rg_i': b_rg_i[l],
            'rg_lambda': rg_lambda[l], 'w_out_b': w_out_b[l], 'w_o': w_o[l],
            'g_ffn2_pre': g_ffn2_pre[l], 'g_ffn2_post': g_ffn2_post[l],
            'w_ffn2_gate': w_ffn2_gate[l], 'w_ffn2_up': w_ffn2_up[l], 'w_ffn2_down': w_ffn2_down[l],
        }
        za = jnp.zeros((Bp, CONV_A_W - 1, D_CONV), xp.dtype)
        zb = jnp.zeros((Bp, CONV_B_W - 1, D_RNN), xp.dtype)
        zh = jnp.zeros((Bp, D_RNN), jnp.float32)
        xp, nba, nbb, nh = _layer(xp, za, zb, zh, p)
        pa.append(nba.astype(state_conv_a.dtype))
        pb.append(nbb.astype(state_conv_b.dtype))
        ph.append(nh.astype(state_rglru.dtype))
        xs, nba, nbb, nh = _layer(xs, state_conv_a[l], state_conv_b[l], state_rglru[l].astype(jnp.float32), p)
        sa.append(nba.astype(state_conv_a.dtype))
        sb.append(nbb.astype(state_conv_b.dtype))
        sh.append(nh.astype(state_rglru.dtype))
    y_prompt = xp[:, N_META:]
    y_sample = xs
    return (y_prompt, y_sample, jnp.stack(pa), jnp.stack(pb), jnp.stack(ph), jnp.stack(sa), jnp.stack(sb), jnp.stack(sh))
```

```python
import functools

import jax
import jax.numpy as jnp
from jax import lax
from jax.experimental import pallas as pl
from jax.experimental.pallas import tpu as pltpu

F32 = jnp.float32
BF16 = jnp.bfloat16

EPS = 1e-6
RG_C = 8.0
N_RNN_HEADS = 16
HEADS_PER_GROUP = 4
SUBLANES = 8
ROW_TILE = 640
ROW_CHUNK = 32
FF_CHUNK = 256
VMEM_LIMIT_BYTES = 60 * 1024 * 1024


def _dot(a, b):
    return jnp.dot(a, b, preferred_element_type=F32)


def _rms_rows(x, gain):
    ms = jnp.mean(x * x, axis=-1, keepdims=True)
    return x * lax.rsqrt(ms + EPS) * gain


def _ffn_kernel(x_ref, gpre_ref, gpost_ref, wg_ref, wu_ref, wd_ref, o_ref,
                xn_ref, g_ref, h_ref, y_ref):
    tm = x_ref.shape[0]
    d_ff = wg_ref.shape[1]
    for r in range(0, tm, ROW_CHUNK):
        rows = pl.ds(r, ROW_CHUNK)
        xn_ref[rows, :] = _rms_rows(x_ref[rows, :], gpre_ref[...]).astype(BF16)
    for k, c in enumerate(range(0, d_ff, FF_CHUNK)):
        cols = pl.ds(c, FF_CHUNK)
        slot = k % 2
        g_ref[slot] = _dot(xn_ref[...], wg_ref[:, cols])
        u = _dot(xn_ref[...], wu_ref[:, cols])
        g = g_ref[slot]
        h_ref[:, cols] = (g * jax.nn.sigmoid(g) * u).astype(BF16)
    y_ref[...] = _dot(h_ref[...], wd_ref[...])
    for r in range(0, tm, ROW_CHUNK):
        rows = pl.ds(r, ROW_CHUNK)
        o_ref[rows, :] = x_ref[rows, :] + 0.5 * _rms_rows(y_ref[rows, :], gpost_ref[...])


def _resident(shape):
    zeros = (0,) * len(shape)
    return pl.BlockSpec(shape, lambda i: zeros, pipeline_mode=pl.Buffered(1))


def _ffn(x, g_pre, g_post, w_gate, w_up, w_down):
    rows, d = x.shape
    d_ff = w_gate.shape[1]
    assert rows % ROW_TILE == 0 and d_ff % FF_CHUNK == 0
    row_spec = pl.BlockSpec((ROW_TILE, d), lambda i: (i, 0))
    return pl.pallas_call(
        _ffn_kernel,
        grid=(rows // ROW_TILE,),
        in_specs=[row_spec, _resident((1, d)), _resident((1, d)),
                  _resident((d, d_ff)), _resident((d, d_ff)), _resident((d_ff, d))],
        out_specs=row_spec,
        out_shape=jax.ShapeDtypeStruct((rows, d), F32),
        scratch_shapes=[
            pltpu.VMEM((ROW_TILE, d), BF16),
            pltpu.VMEM((2, ROW_TILE, FF_CHUNK), F32),
            pltpu.VMEM((ROW_TILE, d_ff), BF16),
            pltpu.VMEM((ROW_TILE, d), F32),
        ],
        compiler_params=pltpu.CompilerParams(
            dimension_semantics=("arbitrary",), vmem_limit_bytes=VMEM_LIMIT_BYTES),
        name="ffn",
    )(x, g_pre, g_post, w_gate, w_up, w_down)


def _mixer_kernel(x_ref, gpre_ref, gpost_ref, win_ref, caw_ref, wouta_ref, cbw_ref, cbb_ref,
                  wgate_ref, br_ref, bi_ref, lam_ref, woutb_ref, wo_ref,
                  sca_ref, scb_ref, sh_ref,
                  o_ref, pa_ref, pb_ref, ph_ref, sa_ref, sb_ref, shn_ref,
                  hn_ref, t1_ref, cv_ref, m_ref, hs_ref, p_ref, cbh_ref,
                  carry_a_ref, carry_b_ref, carry_h_ref, *, n_sample, batch):
    i = pl.program_id(0)
    is_last = i == pl.num_programs(0) - 1
    tm, d = x_ref.shape
    ka = caw_ref.shape[0]
    kb = cbw_ref.shape[0]
    pad = (kb - 1) * batch
    s0 = tm - n_sample
    assert s0 % batch == 0 and s0 >= pad

    def win(j):
        return win_ref[:, pl.ds(j * d, d)]

    @pl.when(i == 0)
    def _():
        carry_a_ref[...] = jnp.zeros_like(carry_a_ref)
        carry_b_ref[...] = jnp.zeros_like(carry_b_ref)
        carry_h_ref[...] = jnp.zeros_like(carry_h_ref)

    for r in range(0, tm, ROW_CHUNK):
        rows = pl.ds(r, ROW_CHUNK)
        hn_ref[rows, :] = _rms_rows(x_ref[rows, :], gpre_ref[...]).astype(BF16)

    t1_ref[...] = _dot(hn_ref[...], win(1))
    cv_ref[pl.ds(pad, tm), :] = t1_ref[...] * _dot(hn_ref[...], win(2))
    hist_a = (ka - 1) * batch
    cv_ref[pl.ds(pad - hist_a, hist_a), :] = carry_a_ref[...]
    carry_a_ref[...] = cv_ref[pl.ds(tm + pad - hist_a, hist_a), :]
    pa_ref[...] = cv_ref[pl.ds(s0 + pad - hist_a, hist_a), :]
    t1_ref[...] = _dot(hn_ref[...], win(0))
    for r in range(0, tm, ROW_CHUNK):
        conv = caw_ref[ka - 1:ka, :] * cv_ref[pl.ds(pad + r, ROW_CHUNK), :]
        for k in range(ka - 1):
            shift = (ka - 1 - k) * batch
            conv = conv + caw_ref[k:k + 1, :] * cv_ref[pl.ds(pad + r - shift, ROW_CHUNK), :]
        if r >= s0:
            srows = pl.ds(r - s0, ROW_CHUNK)
            v_new = cv_ref[pl.ds(pad + r, ROW_CHUNK), :]
            conv_s = caw_ref[ka - 1:ka, :] * v_new
            for k in range(ka - 1):
                conv_s = conv_s + caw_ref[k:k + 1, :] * sca_ref[srows, pl.ds(k * d, d)]
            conv = jnp.where(is_last, conv_s, conv)
            for k in range(1, ka - 1):
                sa_ref[srows, pl.ds((k - 1) * d, d)] = sca_ref[srows, pl.ds(k * d, d)]
            sa_ref[srows, pl.ds((ka - 2) * d, d)] = v_new
        p_ref[pl.ds(r, ROW_CHUNK), :] = (t1_ref[pl.ds(r, ROW_CHUNK), :] * conv).astype(BF16)
    t1_ref[...] = _dot(p_ref[...], wouta_ref[...])
    m_ref[...] = jax.nn.sigmoid(_dot(hn_ref[...], win(5))) * t1_ref[...]

    cv_ref[pl.ds(pad, tm), :] = _dot(hn_ref[...], win(3))
    cv_ref[pl.ds(0, pad), :] = carry_b_ref[...]
    carry_b_ref[...] = cv_ref[pl.ds(tm, pad), :]
    pb_ref[...] = cv_ref[pl.ds(s0, pad), :]
    for r in range(0, tm, ROW_CHUNK):
        conv = cbb_ref[...] + cbw_ref[kb - 1:kb, :] * cv_ref[pl.ds(pad + r, ROW_CHUNK), :]
        for k in range(kb - 1):
            shift = (kb - 1 - k) * batch
            conv = conv + cbw_ref[k:k + 1, :] * cv_ref[pl.ds(pad + r - shift, ROW_CHUNK), :]
        if r >= s0:
            srows = pl.ds(r - s0, ROW_CHUNK)
            x_new = cv_ref[pl.ds(pad + r, ROW_CHUNK), :]
            conv_s = cbb_ref[...] + cbw_ref[kb - 1:kb, :] * x_new
            for k in range(kb - 1):
                conv_s = conv_s + cbw_ref[k:k + 1, :] * scb_ref[srows, pl.ds(k * d, d)]
            conv = jnp.where(is_last, conv_s, conv)
            for k in range(1, kb - 1):
                sb_ref[srows, pl.ds((k - 1) * d, d)] = scb_ref[srows, pl.ds(k * d, d)]
            sb_ref[srows, pl.ds((kb - 2) * d, d)] = x_new
        t1_ref[pl.ds(r, ROW_CHUNK), :] = conv
        cbh_ref[pl.ds(r, ROW_CHUNK), :] = conv.astype(BF16)

    n_groups = wgate_ref.shape[0]
    gw = d // n_groups
    neg_c_softplus = -RG_C * jax.nn.softplus(-lam_ref[...])
    for j in range(n_groups):
        cols = pl.ds(j * gw, gw)
        ri_ref = cv_ref.at[pl.ds(0, tm), pl.ds((j % 2) * 2 * gw, 2 * gw)]
        ri_ref[...] = _dot(cbh_ref[:, cols], wgate_ref[j])
        scale = neg_c_softplus[:, j * gw:(j + 1) * gw]
        b_r = br_ref[:, cols]
        b_i = bi_ref[:, cols]
        h = carry_h_ref[:, cols]
        for t in range(tm // batch):
            rows = pl.ds(t * batch, batch)
            r_gate = jax.nn.sigmoid(ri_ref[rows, pl.ds(0, gw)] + b_r)
            i_gate = jax.nn.sigmoid(ri_ref[rows, pl.ds(gw, gw)] + b_i)
            log_a = scale * r_gate
            a = jnp.exp(log_a)
            u = jnp.sqrt(1.0 - a * a) * (i_gate * t1_ref[rows, cols])
            if t * batch >= s0:
                h_prev = jnp.where(is_last, sh_ref[pl.ds(t * batch - s0, batch), cols], h)
            else:
                h_prev = h
            h = a * h_prev + u
            hs_ref[rows, cols] = h
            if t * batch >= s0:
                shn_ref[pl.ds(t * batch - s0, batch), cols] = h
            if (t + 1) * batch == s0:
                ph_ref[:, cols] = h
        carry_h_ref[:, cols] = h

    t1_ref[...] = _dot(hn_ref[...], win(4))
    for r in range(0, tm, ROW_CHUNK):
        rows = pl.ds(r, ROW_CHUNK)
        p_ref[rows, :] = (jax.nn.gelu(t1_ref[rows, :], approximate=True) * hs_ref[rows, :]).astype(BF16)
    t1_ref[...] = _dot(p_ref[...], woutb_ref[...])
    m_ref[...] = m_ref[...] + jax.nn.sigmoid(_dot(hn_ref[...], win(6))) * t1_ref[...]

    for r in range(0, tm, ROW_CHUNK):
        rows = pl.ds(r, ROW_CHUNK)
        p_ref[rows, :] = m_ref[rows, :].astype(BF16)
    t1_ref[...] = _dot(p_ref[...], wo_ref[...])
    for r in range(0, tm, ROW_CHUNK):
        rows = pl.ds(r, ROW_CHUNK)
        o_ref[rows, :] = x_ref[rows, :] + _rms_rows(t1_ref[rows, :], gpost_ref[...])


def _mixer(x, g_pre, g_post, w_in, conv_a_w, w_out_a, conv_b_w, conv_b_b, w_gate, b_r, b_i, lam,
           w_out_b, w_o, state_a, state_b, state_h, batch):
    rows, d = x.shape
    n_sample = state_h.shape[0]
    ka, kb = conv_a_w.shape[0], conv_b_w.shape[0]
    assert rows % ROW_TILE == 0 and n_sample % ROW_CHUNK == 0 and batch == SUBLANES
    assert (ROW_TILE - n_sample) % ROW_CHUNK == 0
    row_spec = pl.BlockSpec((ROW_TILE, d), lambda i: (i, 0))
    operands = (x, g_pre, g_post, w_in, conv_a_w, w_out_a, conv_b_w, conv_b_b, w_gate, b_r, b_i, lam,
                w_out_b, w_o, state_a, state_b, state_h)
    out_shapes = (
        jax.ShapeDtypeStruct((rows, d), F32),
        jax.ShapeDtypeStruct(((ka - 1) * batch, d), F32),
        jax.ShapeDtypeStruct(((kb - 1) * batch, d), F32),
        jax.ShapeDtypeStruct((batch, d), F32),
        jax.ShapeDtypeStruct(state_a.shape, F32),
        jax.ShapeDtypeStruct(state_b.shape, F32),
        jax.ShapeDtypeStruct(state_h.shape, F32),
    )
    return pl.pallas_call(
        functools.partial(_mixer_kernel, n_sample=n_sample, batch=batch),
        grid=(rows // ROW_TILE,),
        in_specs=[row_spec] + [_resident(a.shape) for a in operands[1:]],
        out_specs=[row_spec] + [pl.BlockSpec(s.shape, lambda i: (0, 0)) for s in out_shapes[1:]],
        out_shape=out_shapes,
        scratch_shapes=[
            pltpu.VMEM((ROW_TILE, d), BF16),
            pltpu.VMEM((ROW_TILE, d), F32),
            pltpu.VMEM((ROW_TILE + (kb - 1) * batch, d), F32),
            pltpu.VMEM((ROW_TILE, d), F32),
            pltpu.VMEM((ROW_TILE, d), F32),
            pltpu.VMEM((ROW_TILE, d), BF16),
            pltpu.VMEM((ROW_TILE, d), BF16),
            pltpu.VMEM(((ka - 1) * batch, d), F32),
            pltpu.VMEM(((kb - 1) * batch, d), F32),
            pltpu.VMEM((batch, d), F32),
        ],
        compiler_params=pltpu.CompilerParams(
            dimension_semantics=("arbitrary",), vmem_limit_bytes=VMEM_LIMIT_BYTES),
        name="mixer",
    )(*operands)


def _block_diag_gates(w_r, w_i):
    n_heads, hd, _ = w_r.shape
    g = HEADS_PER_GROUP
    eye = jnp.eye(g, dtype=w_r.dtype)[None, :, None, :, None]

    def bd(w):
        w = w.reshape(n_heads // g, g, hd, 1, hd) * eye
        return w.reshape(n_heads // g, g * hd, g * hd)

    return jnp.concatenate([bd(w_r), bd(w_i)], axis=-1).astype(BF16)


def kernel(x_prompt, x_sample, state_conv_a, state_conv_b, state_rglru, meta_tokens, g_ffn1_pre, g_ffn1_post, w_ffn1_gate, w_ffn1_up, w_ffn1_down, g_mix_pre, g_mix_post, w_in, conv_a_w, w_out_a, conv_b_w, conv_b_b, w_rg_r, b_rg_r, w_rg_i, b_rg_i, rg_lambda, w_out_b, w_o, g_ffn2_pre, g_ffn2_post, w_ffn2_gate, w_ffn2_up, w_ffn2_down):
    batch, seq, d = x_prompt.shape
    n_sample = x_sample.shape[0]
    n_meta = meta_tokens.shape[0]
    depth = w_in.shape[0]
    t_total = n_meta + seq
    n_prompt_rows = t_total * batch

    meta = jnp.broadcast_to(meta_tokens.astype(x_prompt.dtype)[:, None, :], (n_meta, batch, d))
    xp = jnp.concatenate([meta, jnp.swapaxes(x_prompt, 0, 1)], axis=0).reshape(n_prompt_rows, d)
    x = jnp.concatenate([xp, x_sample.reshape(n_sample, d)], axis=0)

    row = lambda v: v.reshape(1, -1)
    pa, pb, ph, sa, sb, sh = [], [], [], [], [], []
    for l in range(depth):
        x = _ffn(x, row(g_ffn1_pre[l]), row(g_ffn1_post[l]),
                 w_ffn1_gate[l].astype(BF16), w_ffn1_up[l].astype(BF16), w_ffn1_down[l].astype(BF16))
        x, npa, npb, nph, nsa, nsb, nsh = _mixer(
            x, row(g_mix_pre[l]), row(g_mix_post[l]), w_in[l].astype(BF16),
            conv_a_w[l], w_out_a[l].astype(BF16), conv_b_w[l], row(conv_b_b[l]),
            _block_diag_gates(w_rg_r[l], w_rg_i[l]), row(b_rg_r[l]), row(b_rg_i[l]), row(rg_lambda[l]),
            w_out_b[l].astype(BF16), w_o[l].astype(BF16),
            state_conv_a[l].reshape(n_sample, -1), state_conv_b[l].reshape(n_sample, -1),
            state_rglru[l].astype(F32), batch)
        x = _ffn(x, row(g_ffn2_pre[l]), row(g_ffn2_post[l]),
                 w_ffn2_gate[l].astype(BF16), w_ffn2_up[l].astype(BF16), w_ffn2_down[l].astype(BF16))
        pa.append(jnp.swapaxes(npa.reshape(-1, batch, d), 0, 1).astype(state_conv_a.dtype))
        pb.append(jnp.swapaxes(npb.reshape(-1, batch, d), 0, 1).astype(state_conv_b.dtype))
        ph.append(nph.astype(state_rglru.dtype))
        sa.append(nsa.reshape(state_conv_a.shape[1:]).astype(state_conv_a.dtype))
        sb.append(nsb.reshape(state_conv_b.shape[1:]).astype(state_conv_b.dtype))
        sh.append(nsh.astype(state_rglru.dtype))

    y_prompt = jnp.swapaxes(x[:n_prompt_rows].reshape(t_total, batch, d)[n_meta:], 0, 1)
    y_sample = x[n_prompt_rows:].reshape(x_sample.shape)
    return (y_prompt, y_sample, jnp.stack(pa), jnp.stack(pb), jnp.stack(ph),
            jnp.stack(sa), jnp.stack(sb), jnp.stack(sh))
```

```python
import functools

import jax
import jax.numpy as jnp
from jax import lax
from jax.experimental import pallas as pl
from jax.experimental.pallas import tpu as pltpu

F32 = jnp.float32
BF16 = jnp.bfloat16

EPS = 1e-6
RG_C = 8.0
HEADS_PER_GROUP = 4
SUBLANES = 8
LANES = 128
TIME_TILE = 64
ROW_CHUNK = 32
FF_CHUNK = 256
VMEM_LIMIT_BYTES = 60 * 1024 * 1024


def _dot(a, b):
    return jnp.dot(a, b, preferred_element_type=F32)


def _rms_rows(x, gain):
    ms = jnp.mean(x * x, axis=-1, keepdims=True)
    return x * lax.rsqrt(ms + EPS) * gain


def _resident(shape):
    zeros = (0,) * len(shape)
    return pl.BlockSpec(shape, lambda i: zeros, pipeline_mode=pl.Buffered(1))


def _ffn_kernel(*refs, special_in, special_out, special_step):
    refs = list(refs)
    x_ref = refs.pop(0)
    xs_ref = refs.pop(0) if special_in else None
    gpre_ref, gpost_ref, wg_ref, wu_ref, wd_ref, o_ref = refs[:6]
    refs = refs[6:]
    os_ref = refs.pop(0) if special_out else None
    xn_ref, g_ref, h_ref, y_ref = refs
    nb, tt, _ = x_ref.shape
    d_ff = wg_ref.shape[1]
    is_special = pl.program_id(0) == special_step

    def load_x(b, rows):
        x = x_ref[b, rows, :]
        if special_in:
            x = jnp.where(is_special, xs_ref[b, rows, :], x)
        return x

    for b in range(nb):
        for t0 in range(0, tt, ROW_CHUNK):
            xn_ref[pl.ds(b * tt + t0, ROW_CHUNK), :] = _rms_rows(
                load_x(b, pl.ds(t0, ROW_CHUNK)), gpre_ref[...]).astype(BF16)
    for k, c in enumerate(range(0, d_ff, FF_CHUNK)):
        cols = pl.ds(c, FF_CHUNK)
        slot = k % 2
        g_ref[slot] = _dot(xn_ref[...], wg_ref[:, cols])
        u = _dot(xn_ref[...], wu_ref[:, cols])
        g = g_ref[slot]
        h_ref[:, cols] = (g * jax.nn.sigmoid(g) * u).astype(BF16)
    y_ref[...] = _dot(h_ref[...], wd_ref[...])
    for b in range(nb):
        for t0 in range(0, tt, ROW_CHUNK):
            rows = pl.ds(t0, ROW_CHUNK)
            o_ref[b, rows, :] = load_x(b, rows) + 0.5 * _rms_rows(
                y_ref[pl.ds(b * tt + t0, ROW_CHUNK), :], gpost_ref[...])
    if special_out:
        @pl.when(is_special)
        def _():
            os_ref[...] = o_ref[...]


def _ffn(x, x_special, g_pre, g_post, w_gate, w_up, w_down, *, n_main, special_out):
    nb, _, d = x.shape
    d_ff = w_gate.shape[1]
    tm = nb * TIME_TILE
    n_steps = n_main + 1
    special_in = x_special is not None
    assert d_ff % FF_CHUNK == 0 and TIME_TILE % ROW_CHUNK == 0
    assert x.shape[1] == (n_main if special_in else n_steps) * TIME_TILE
    block = (nb, TIME_TILE, d)
    if special_in:
        special_step = n_main
        in_specs = [pl.BlockSpec(block, lambda i: (0, jnp.minimum(i, n_main - 1), 0)), _resident(block)]
        operands = [x, x_special]
    else:
        special_step = 0
        in_specs = [pl.BlockSpec(block, lambda i: (0, (i + n_main) % n_steps, 0))]
        operands = [x]
    if special_out:
        assert special_step == 0
        out_specs = [pl.BlockSpec(block, lambda i: (0, jnp.maximum(i - 1, 0), 0)),
                     pl.BlockSpec(block, lambda i: (0, 0, 0))]
        out_shape = [jax.ShapeDtypeStruct((nb, n_main * TIME_TILE, d), F32),
                     jax.ShapeDtypeStruct(block, F32)]
    else:
        out_index = (lambda i: (0, i, 0)) if special_in else (lambda i: (0, (i + n_main) % n_steps, 0))
        out_specs = pl.BlockSpec(block, out_index)
        out_shape = jax.ShapeDtypeStruct((nb, n_steps * TIME_TILE, d), F32)
    return pl.pallas_call(
        functools.partial(_ffn_kernel, special_in=special_in, special_out=special_out,
                          special_step=special_step),
        grid=(n_steps,),
        in_specs=in_specs + [_resident((1, d)), _resident((1, d)),
                             _resident((d, d_ff)), _resident((d, d_ff)), _resident((d_ff, d))],
        out_specs=out_specs,
        out_shape=out_shape,
        scratch_shapes=[
            pltpu.VMEM((tm, d), BF16),
            pltpu.VMEM((2, tm, FF_CHUNK), F32),
            pltpu.VMEM((tm, d_ff), BF16),
            pltpu.VMEM((tm, d), F32),
        ],
        compiler_params=pltpu.CompilerParams(
            dimension_semantics=("arbitrary",), vmem_limit_bytes=VMEM_LIMIT_BYTES),
        name="ffn",
    )(*operands, g_pre, g_post, w_gate, w_up, w_down)


def _mixer_kernel(x_ref, gpre_ref, gpost_ref, win_ref, caw_ref, wouta_ref, cbw_ref, cbb_ref,
                  wgate_ref, br_ref, bi_ref, lam_ref, woutb_ref, wo_ref,
                  sca_ref, scb_ref, sh_ref,
                  o_ref, pa_ref, pb_ref, ph_ref, sa_ref, sb_ref, shn_ref,
                  xt_ref, hn_ref, t1_ref, cv_ref, m_ref, hs_ref, p_ref, cbh_ref,
                  carry_a_ref, carry_b_ref, carry_h_ref, new_a_ref, new_b_ref, new_h_ref, *, n_meta):
    is_special = pl.program_id(0) == 0
    batch, tt, d = x_ref.shape
    tm = batch * tt
    n_sample = sh_ref.shape[0]
    n_slabs = d // LANES
    ka = caw_ref.shape[0]
    kb = cbw_ref.shape[0]
    pad = (kb - 1) * batch
    hist_a = (ka - 1) * batch
    s0 = n_meta * batch
    s1 = s0 + n_sample
    assert s0 >= pad and s1 <= tm and s0 % ROW_CHUNK == 0 and n_sample % ROW_CHUNK == 0

    def win(j):
        return win_ref[:, pl.ds(j * d, d)]

    def time_major(b, t0):
        return pl.ds(t0 * batch + b, SUBLANES, stride=batch)

    @pl.when(is_special)
    def _():
        carry_a_ref[...] = jnp.zeros_like(carry_a_ref)
        carry_b_ref[...] = jnp.zeros_like(carry_b_ref)
        carry_h_ref[...] = jnp.zeros_like(carry_h_ref)

    for b in range(batch):
        for t0 in range(0, tt, SUBLANES):
            y = _rms_rows(x_ref[b, pl.ds(t0, SUBLANES), :], gpre_ref[...])
            for j in range(n_slabs):
                xt_ref[j, time_major(b, t0), :] = y[:, j * LANES:(j + 1) * LANES]
    for r in range(0, tm, ROW_CHUNK):
        rows = pl.ds(r, ROW_CHUNK)
        for j in range(n_slabs):
            hn_ref[rows, pl.ds(j * LANES, LANES)] = xt_ref[j, rows, :].astype(BF16)

    t1_ref[...] = _dot(hn_ref[...], win(1))
    cv_ref[pl.ds(pad, tm), :] = t1_ref[...] * _dot(hn_ref[...], win(2))
    cv_ref[pl.ds(pad - hist_a, hist_a), :] = carry_a_ref[...]
    carry_a_ref[...] = jnp.where(is_special, cv_ref[pl.ds(pad + s0 - hist_a, hist_a), :],
                                 cv_ref[pl.ds(pad + tm - hist_a, hist_a), :])
    pa_ref[...] = cv_ref[pl.ds(pad + tm - hist_a, hist_a), :]
    t1_ref[...] = _dot(hn_ref[...], win(0))
    for r in range(0, tm, ROW_CHUNK):
        conv = caw_ref[ka - 1:ka, :] * cv_ref[pl.ds(pad + r, ROW_CHUNK), :]
        for k in range(ka - 1):
            shift = (ka - 1 - k) * batch
            conv = conv + caw_ref[k:k + 1, :] * cv_ref[pl.ds(pad + r - shift, ROW_CHUNK), :]
        if s0 <= r < s1:
            srows = pl.ds(r - s0, ROW_CHUNK)
            v_new = cv_ref[pl.ds(pad + r, ROW_CHUNK), :]
            conv_s = caw_ref[ka - 1:ka, :] * v_new
            for k in range(ka - 1):
                conv_s = conv_s + caw_ref[k:k + 1, :] * sca_ref[srows, pl.ds(k * d, d)]
            conv = jnp.where(is_special, conv_s, conv)
            new_a_ref[srows, :] = v_new
        p_ref[pl.ds(r, ROW_CHUNK), :] = (t1_ref[pl.ds(r, ROW_CHUNK), :] * conv).astype(BF16)
    t1_ref[...] = _dot(p_ref[...], wouta_ref[...])
    m_ref[...] = jax.nn.sigmoid(_dot(hn_ref[...], win(5))) * t1_ref[...]

    cv_ref[pl.ds(pad, tm), :] = _dot(hn_ref[...], win(3))
    cv_ref[pl.ds(0, pad), :] = carry_b_ref[...]
    carry_b_ref[...] = jnp.where(is_special, cv_ref[pl.ds(s0, pad), :], cv_ref[pl.ds(tm, pad), :])
    pb_ref[...] = cv_ref[pl.ds(tm, pad), :]
    for r in range(0, tm, ROW_CHUNK):
        conv = cbb_ref[...] + cbw_ref[kb - 1:kb, :] * cv_ref[pl.ds(pad + r, ROW_CHUNK), :]
        for k in range(kb - 1):
            shift = (kb - 1 - k) * batch
            conv = conv + cbw_ref[k:k + 1, :] * cv_ref[pl.ds(pad + r - shift, ROW_CHUNK), :]
        if s0 <= r < s1:
            srows = pl.ds(r - s0, ROW_CHUNK)
            x_new = cv_ref[pl.ds(pad + r, ROW_CHUNK), :]
            conv_s = cbb_ref[...] + cbw_ref[kb - 1:kb, :] * x_new
            for k in range(kb - 1):
                conv_s = conv_s + cbw_ref[k:k + 1, :] * scb_ref[srows, pl.ds(k * d, d)]
            conv = jnp.where(is_special, conv_s, conv)
            new_b_ref[srows, :] = x_new
        t1_ref[pl.ds(r, ROW_CHUNK), :] = conv
        cbh_ref[pl.ds(r, ROW_CHUNK), :] = conv.astype(BF16)

    n_groups = wgate_ref.shape[0]
    gw = d // n_groups
    neg_c_softplus = -RG_C * jax.nn.softplus(-lam_ref[...])
    for j in range(n_groups):
        cols = pl.ds(j * gw, gw)
        ri_ref = cv_ref.at[pl.ds(0, tm), pl.ds((j % 2) * 2 * gw, 2 * gw)]
        ri_ref[...] = _dot(cbh_ref[:, cols], wgate_ref[j])
        scale = neg_c_softplus[:, j * gw:(j + 1) * gw]
        b_r = br_ref[:, cols]
        b_i = bi_ref[:, cols]
        h = carry_h_ref[:, cols]
        for t in range(tt):
            rows = pl.ds(t * batch, batch)
            r_gate = jax.nn.sigmoid(ri_ref[rows, pl.ds(0, gw)] + b_r)
            i_gate = jax.nn.sigmoid(ri_ref[rows, pl.ds(gw, gw)] + b_i)
            log_a = scale * r_gate
            a = jnp.exp(log_a)
            u = jnp.sqrt(1.0 - a * a) * (i_gate * t1_ref[rows, cols])
            if s0 <= t * batch < s1:
                srows = pl.ds(t * batch - s0, batch)
                h = a * jnp.where(is_special, sh_ref[srows, cols], h) + u
                new_h_ref[srows, cols] = h
            else:
                h = a * h + u
            hs_ref[rows, cols] = h
            if (t + 1) * batch == s0:
                h_meta = h
        carry_h_ref[:, cols] = jnp.where(is_special, h_meta, h)
        ph_ref[:, cols] = h

    t1_ref[...] = _dot(hn_ref[...], win(4))
    for r in range(0, tm, ROW_CHUNK):
        rows = pl.ds(r, ROW_CHUNK)
        p_ref[rows, :] = (jax.nn.gelu(t1_ref[rows, :], approximate=True) * hs_ref[rows, :]).astype(BF16)
    t1_ref[...] = _dot(p_ref[...], woutb_ref[...])
    m_ref[...] = m_ref[...] + jax.nn.sigmoid(_dot(hn_ref[...], win(6))) * t1_ref[...]

    for r in range(0, tm, ROW_CHUNK):
        rows = pl.ds(r, ROW_CHUNK)
        p_ref[rows, :] = m_ref[rows, :].astype(BF16)
    mo = _dot(p_ref[...], wo_ref[...])
    for j in range(n_slabs):
        xt_ref[j] = mo[:, j * LANES:(j + 1) * LANES]
    for b in range(batch):
        for t0 in range(0, tt, SUBLANES):
            y = jnp.concatenate([xt_ref[j, time_major(b, t0), :] for j in range(n_slabs)], axis=1)
            rows = pl.ds(t0, SUBLANES)
            o_ref[b, rows, :] = x_ref[b, rows, :] + _rms_rows(y, gpost_ref[...])

    @pl.when(is_special)
    def _():
        sa_ref[:, pl.ds(0, (ka - 2) * d)] = sca_ref[:, pl.ds(d, (ka - 2) * d)]
        sa_ref[:, pl.ds((ka - 2) * d, d)] = new_a_ref[...]
        sb_ref[:, pl.ds(0, (kb - 2) * d)] = scb_ref[:, pl.ds(d, (kb - 2) * d)]
        sb_ref[:, pl.ds((kb - 2) * d, d)] = new_b_ref[...]
        shn_ref[...] = new_h_ref[...]


def _mixer(x, g_pre, g_post, w_in, conv_a_w, w_out_a, conv_b_w, conv_b_b, w_gate, b_r, b_i, lam,
           w_out_b, w_o, state_a, state_b, state_h, *, n_main, n_meta):
    batch, t_pad, d = x.shape
    n_steps = n_main + 1
    tm = batch * TIME_TILE
    ka, kb = conv_a_w.shape[0], conv_b_w.shape[0]
    assert t_pad == n_steps * TIME_TILE and batch == SUBLANES and d % LANES == 0
    block = (batch, TIME_TILE, d)
    block_spec = pl.BlockSpec(block, lambda i: (0, (i + n_main) % n_steps, 0))
    operands = (x, g_pre, g_post, w_in, conv_a_w, w_out_a, conv_b_w, conv_b_b, w_gate, b_r, b_i, lam,
                w_out_b, w_o, state_a, state_b, state_h)
    out_shapes = (
        jax.ShapeDtypeStruct(x.shape, F32),
        jax.ShapeDtypeStruct(((ka - 1) * batch, d), F32),
        jax.ShapeDtypeStruct(((kb - 1) * batch, d), F32),
        jax.ShapeDtypeStruct((batch, d), F32),
        jax.ShapeDtypeStruct(state_a.shape, F32),
        jax.ShapeDtypeStruct(state_b.shape, F32),
        jax.ShapeDtypeStruct(state_h.shape, F32),
    )
    return pl.pallas_call(
        functools.partial(_mixer_kernel, n_meta=n_meta),
        grid=(n_steps,),
        in_specs=[block_spec] + [_resident(a.shape) for a in operands[1:]],
        out_specs=[block_spec] + [pl.BlockSpec(s.shape, lambda i: (0, 0)) for s in out_shapes[1:]],
        out_shape=out_shapes,
        scratch_shapes=[
            pltpu.VMEM((d // LANES, tm, LANES), F32),
            pltpu.VMEM((tm, d), BF16),
            pltpu.VMEM((tm, d), F32),
            pltpu.VMEM((tm + (kb - 1) * batch, d), F32),
            pltpu.VMEM((tm, d), F32),
            pltpu.VMEM((tm, d), F32),
            pltpu.VMEM((tm, d), BF16),
            pltpu.VMEM((tm, d), BF16),
            pltpu.VMEM(((ka - 1) * batch, d), F32),
            pltpu.VMEM(((kb - 1) * batch, d), F32),
            pltpu.VMEM((batch, d), F32),
            pltpu.VMEM(state_h.shape, F32),
            pltpu.VMEM(state_h.shape, F32),
            pltpu.VMEM(state_h.shape, F32),
        ],
        compiler_params=pltpu.CompilerParams(
            dimension_semantics=("arbitrary",), vmem_limit_bytes=VMEM_LIMIT_BYTES),
        name="mixer",
    )(*operands)


def _block_diag_gates(w_r, w_i):
    n_heads, hd, _ = w_r.shape
    g = HEADS_PER_GROUP
    eye = jnp.eye(g, dtype=w_r.dtype)[None, :, None, :, None]

    def bd(w):
        w = w.reshape(n_heads // g, g, hd, 1, hd) * eye
        return w.reshape(n_heads // g, g * hd, g * hd)

    return jnp.concatenate([bd(w_r), bd(w_i)], axis=-1).astype(BF16)


def kernel(x_prompt, x_sample, state_conv_a, state_conv_b, state_rglru, meta_tokens, g_ffn1_pre, g_ffn1_post, w_ffn1_gate, w_ffn1_up, w_ffn1_down, g_mix_pre, g_mix_post, w_in, conv_a_w, w_out_a, conv_b_w, conv_b_b, w_rg_r, b_rg_r, w_rg_i, b_rg_i, rg_lambda, w_out_b, w_o, g_ffn2_pre, g_ffn2_post, w_ffn2_gate, w_ffn2_up, w_ffn2_down):
    batch, seq, d = x_prompt.shape
    n_sample = x_sample.shape[0]
    n_meta = meta_tokens.shape[0]
    depth = w_in.shape[0]
    assert seq % TIME_TILE == 0 and n_sample % batch == 0
    n_main = seq // TIME_TILE
    t_sample = n_sample // batch
    assert n_meta + t_sample <= TIME_TILE

    meta = jnp.broadcast_to(meta_tokens.astype(x_prompt.dtype)[None], (batch, n_meta, d))
    samples = jnp.swapaxes(x_sample.reshape(t_sample, batch, d), 0, 1)
    fill = jnp.zeros((batch, TIME_TILE - n_meta - t_sample, d), x_prompt.dtype)
    x_special = jnp.concatenate([meta, samples, fill], axis=1)

    row = lambda v: v.reshape(1, -1)
    x = x_prompt
    pa, pb, ph, sa, sb, sh = [], [], [], [], [], []
    for l in range(depth):
        x = _ffn(x, x_special if l == 0 else None, row(g_ffn1_pre[l]), row(g_ffn1_post[l]),
                 w_ffn1_gate[l].astype(BF16), w_ffn1_up[l].astype(BF16), w_ffn1_down[l].astype(BF16),
                 n_main=n_main, special_out=False)
        x, npa, npb, nph, nsa, nsb, nsh = _mixer(
            x, row(g_mix_pre[l]), row(g_mix_post[l]), w_in[l].astype(BF16),
            conv_a_w[l], w_out_a[l].astype(BF16), conv_b_w[l], row(conv_b_b[l]),
            _block_diag_gates(w_rg_r[l], w_rg_i[l]), row(b_rg_r[l]), row(b_rg_i[l]), row(rg_lambda[l]),
            w_out_b[l].astype(BF16), w_o[l].astype(BF16),
            state_conv_a[l].reshape(n_sample, -1), state_conv_b[l].reshape(n_sample, -1),
            state_rglru[l].astype(F32), n_main=n_main, n_meta=n_meta)
        x = _ffn(x, None, row(g_ffn2_pre[l]), row(g_ffn2_post[l]),
                 w_ffn2_gate[l].astype(BF16), w_ffn2_up[l].astype(BF16), w_ffn2_down[l].astype(BF16),
                 n_main=n_main, special_out=(l == depth - 1))
        pa.append(jnp.swapaxes(npa.reshape(-1, batch, d), 0, 1).astype(state_conv_a.dtype))
        pb.append(jnp.swapaxes(npb.reshape(-1, batch, d), 0, 1).astype(state_conv_b.dtype))
        ph.append(nph.astype(state_rglru.dtype))
        sa.append(nsa.reshape(state_conv_a.shape[1:]).astype(state_conv_a.dtype))
        sb.append(nsb.reshape(state_conv_b.shape[1:]).astype(state_conv_b.dtype))
        sh.append(nsh.astype(state_rglru.dtype))

    y_prompt, y_special = x
    y_sample = jnp.swapaxes(y_special[:, n_meta:n_meta + t_sample], 0, 1).reshape(x_sample.shape)
    return (y_prompt, y_sample, jnp.stack(pa), jnp.stack(pb), jnp.stack(ph),
            jnp.stack(sa), jnp.stack(sb), jnp.stack(sh))
```
